```python
import math
import jax, jax.numpy as jnp
from jax import lax
import numpy as np

D_MODEL = 1024
BATCH = 32
SEQ = 2048
DEPTH = 1

CHUNK = 64
LEFT_CHUNKS = 8
BAND = (LEFT_CHUNKS + 1) * CHUNK
MIX_W = D_MODEL
GROUP_W = MIX_W // 2
ATT_HEADS = 8
HEAD_DIM = GROUP_W // ATT_HEADS
CONV_GROUPS = 8
CONV_WIDTH = 3
MAX_REL = 128
IN_COLS = 8 * GROUP_W
LN_EPS = 1e-5
ALPHA = (2.0 * DEPTH) ** 0.25
BETA = (8.0 * DEPTH) ** -0.25

kernel_name = "hybrid_chunkattn_shortconv_deepnorm"


def layer_norm(x, g, b):
    xf = x.astype(jnp.float32)
    mu = jnp.mean(xf, axis=-1, keepdims=True)
    var = jnp.mean(jnp.square(xf - mu), axis=-1, keepdims=True)
    y = (xf - mu) * lax.rsqrt(var + LN_EPS)
    return (y * g.astype(jnp.float32) + b.astype(jnp.float32)).astype(x.dtype)


def chunk_attention(q, k, v, rel_bias):
    bsz, seq, nh, dh = q.shape
    n_chunks = seq // CHUNK
    scale = 1.0 / math.sqrt(dh)
    qi = np.arange(CHUNK)[:, None] + LEFT_CHUNKS * CHUNK
    kj = np.arange(BAND)[None, :]
    rel_idx = np.clip(qi - kj, -MAX_REL, MAX_REL) + MAX_REL
    bias = rel_bias.astype(jnp.float32)[:, rel_idx]
    pad = ((0, 0), (LEFT_CHUNKS * CHUNK, 0), (0, 0), (0, 0))
    kp = jnp.pad(k, pad)
    vp = jnp.pad(v, pad)
    qc = q.reshape(bsz, n_chunks, CHUNK, nh, dh).transpose(1, 0, 2, 3, 4)
    band_off = jnp.arange(BAND) - LEFT_CHUNKS * CHUNK

    def one_chunk(args):
        c, qb = args
        start = c * CHUNK
        kb = lax.dynamic_slice_in_dim(kp, start, BAND, axis=1)
        vb = lax.dynamic_slice_in_dim(vp, start, BAND, axis=1)
        s = jnp.einsum('bqhd,bkhd->bhqk', qb, kb).astype(jnp.float32) * scale + bias[None]
        valid = (start + band_off) >= 0
        s = jnp.where(valid[None, None, None, :], s, jnp.float32(-1e30))
        p = jax.nn.softmax(s, axis=-1).astype(vb.dtype)
        return jnp.einsum('bhqk,bkhd->bqhd', p, vb)

    out = lax.map(one_chunk, (jnp.arange(n_chunks), qc))
    return out.transpose(1, 0, 2, 3, 4).reshape(bsz, seq, nh * dh)


def short_gated_conv(bg, cg, h, conv_w, conv_b):
    u = cg * h
    up = jnp.pad(u, ((0, 0), (CONV_WIDTH - 1, 0), (0, 0)))
    seq = u.shape[1]
    conv = conv_b + sum(conv_w[i] * up[:, i:i + seq] for i in range(CONV_WIDTH))
    return bg * conv


def mixer_layer(x, w_in, rel_bias, conv_w, conv_b, w_out):
    bsz, seq, _ = x.shape
    proj = jnp.einsum('bsd,de->bse', x, w_in)
    q, k, v, z_a, bg, cg, h, z_c = jnp.split(proj, 8, axis=-1)
    heads = lambda t: t.reshape(bsz, seq, ATT_HEADS, HEAD_DIM)
    y_att = chunk_attention(heads(q), heads(k), heads(v), rel_bias) * jax.nn.silu(z_a)
    y_conv = short_gated_conv(bg, cg, h, conv_w, conv_b) * jax.nn.silu(z_c)
    y = jnp.concatenate([y_att, y_conv], axis=-1)
    return jnp.einsum('bse,ed->bsd', y, w_out)


def setup_inputs(seed: int = 0) -> dict:
    key = jax.random.key(seed)
    ks = jax.random.split(key, 12)
    f32 = jnp.float32
    x = jax.random.normal(ks[0], (BATCH, SEQ, D_MODEL), f32)
    ln0_g = 1.0 + 0.01 * jax.random.normal(ks[1], (D_MODEL,), f32)
    ln0_b = 0.01 * jax.random.normal(ks[2], (D_MODEL,), f32)
    w_in = jax.random.normal(ks[3], (DEPTH, D_MODEL, IN_COLS), f32) * D_MODEL ** -0.5
    rel_bias = 0.1 * jax.random.normal(ks[4], (DEPTH, ATT_HEADS, 2 * MAX_REL + 1), f32)
    conv_w = jax.random.normal(ks[5], (DEPTH, CONV_WIDTH, GROUP_W), f32) * CONV_WIDTH ** -0.5
    conv_b = 0.01 * jax.random.normal(ks[6], (DEPTH, GROUP_W), f32)
    w_out = jax.random.normal(ks[7], (DEPTH, MIX_W, D_MODEL), f32) * (MIX_W ** -0.5) * BETA
    ln_g = 1.0 + 0.01 * jax.random.normal(ks[8], (DEPTH, D_MODEL), f32)
    ln_b = 0.01 * jax.random.normal(ks[9], (DEPTH, D_MODEL), f32)
    return {"x": x, "ln0_g": ln0_g, "ln0_b": ln0_b, "w_in": w_in, "rel_bias": rel_bias,
            "conv_w": conv_w, "conv_b": conv_b, "w_out": w_out, "ln_g": ln_g, "ln_b": ln_b}


def reference(x, ln0_g, ln0_b, w_in, rel_bias, conv_w, conv_b, w_out, ln_g, ln_b):
    x = layer_norm(x, ln0_g, ln0_b)
    for l in range(DEPTH):
        h = mixer_layer(x, w_in[l], rel_bias[l], conv_w[l], conv_b[l], w_out[l])
        x = layer_norm(ALPHA * x + h, ln_g[l], ln_b[l])
    return x
```

```python
import functools
import math

import jax
import jax.numpy as jnp
from jax import lax
from jax.experimental import pallas as pl
from jax.experimental.pallas import tpu as pltpu

D_MODEL = 1024
DEPTH = 1
CHUNK = 64
LEFT_CHUNKS = 8
LOOKBACK = LEFT_CHUNKS * CHUNK
BAND = LOOKBACK + CHUNK
GROUP_W = D_MODEL // 2
ATT_HEADS = 8
HEAD_DIM = GROUP_W // ATT_HEADS
CONV_WIDTH = 3
MAX_REL = 128
LN_EPS = 1e-5
ALPHA = (2.0 * DEPTH) ** 0.25
NEG_MASK = -1e30

HEADS_PER_PACK = 4
PACK_W = HEADS_PER_PACK * HEAD_DIM
NUM_PACKS = ATT_HEADS // HEADS_PER_PACK
BIAS_LANES = 1024
CARRY_ROWS = 8

TILE = 512
VMEM_LIMIT_BYTES = 52 * 1024 * 1024


def _layer_norm(v, g, b):
    mu = jnp.mean(v, axis=-1, keepdims=True)
    c = v - mu
    var = jnp.mean(c * c, axis=-1, keepdims=True)
    return c * lax.rsqrt(var + LN_EPS) * g + b


def _silu(z):
    return z * (1.0 / (1.0 + jnp.exp(-z)))


def _fused_kernel(x_ref, ln0g_ref, ln0b_ref, win_ref, relrow_ref, convw_ref, convb_ref,
                  wout_ref, lng_ref, lnb_ref, o_ref,
                  xn_scr, q_scr, k_scr, v_scr, gate_scr, yatt_scr, u_scr, bias_scr):
    b = pl.program_id(0)
    t = pl.program_id(1)
    chunks_per_tile = TILE // CHUNK

    @pl.when(jnp.logical_and(b == 0, t == 0))
    def _build_bias():
        for h in range(ATT_HEADS):
            row = jnp.broadcast_to(relrow_ref[h:h + 1, :], (CHUNK, BIAS_LANES))
            toeplitz = pltpu.roll(row, 0, 1, stride=1, stride_axis=0)
            p, j = divmod(h, HEADS_PER_PACK)
            bias_scr[p, j * CHUNK:(j + 1) * CHUNK, :] = toeplitz[:, :BAND]

    @pl.when(t == 0)
    def _reset_history():
        k_scr[0:LOOKBACK, :] = jnp.zeros((LOOKBACK, GROUP_W), k_scr.dtype)
        v_scr[0:LOOKBACK, :] = jnp.zeros((LOOKBACK, GROUP_W), v_scr.dtype)
        u_scr[0:CARRY_ROWS, :] = jnp.zeros((CARRY_ROWS, GROUP_W), u_scr.dtype)

    @pl.when(t > 0)
    def _shift_history():
        k_scr[0:LOOKBACK, :] = k_scr[TILE:TILE + LOOKBACK, :]
        v_scr[0:LOOKBACK, :] = v_scr[TILE:TILE + LOOKBACK, :]
        u_scr[0:CARRY_ROWS, :] = u_scr[TILE:TILE + CARRY_ROWS, :]

    xn = _layer_norm(x_ref[...], ln0g_ref[...], ln0b_ref[...])
    xn_scr[...] = xn
    xb = xn.astype(jnp.bfloat16)

    def proj(i):
        return jnp.dot(xb, win_ref[:, i * GROUP_W:(i + 1) * GROUP_W],
                       preferred_element_type=jnp.float32)

    q_scr[...] = (proj(0) * (1.0 / math.sqrt(HEAD_DIM))).astype(jnp.bfloat16)
    k_scr[LOOKBACK:LOOKBACK + TILE, :] = proj(1).astype(jnp.bfloat16)
    v_scr[LOOKBACK:LOOKBACK + TILE, :] = proj(2).astype(jnp.bfloat16)
    gate_scr[...] = _silu(proj(3))

    lane_head = lax.broadcasted_iota(jnp.int32, (CHUNK, PACK_W), 1) // HEAD_DIM
    col = lax.broadcasted_iota(jnp.int32, (1, BAND), 1)

    def chunk_body(c, carry):
        r0 = pl.multiple_of(c * CHUNK, CHUNK)
        first_valid = LOOKBACK - (t * chunks_per_tile + c) * CHUNK
        pad_mask = col < first_valid
        for p in range(NUM_PACKS):
            lanes = slice(p * PACK_W, (p + 1) * PACK_W)
            qc = q_scr[pl.ds(r0, CHUNK), lanes]
            qm = jnp.concatenate(
                [jnp.where(lane_head == j, qc, jnp.zeros_like(qc)) for j in range(HEADS_PER_PACK)],
                axis=0)
            kb = k_scr[pl.ds(r0, BAND), lanes]
            s = lax.dot_general(qm, kb, (((1,), (1,)), ((), ())),
                                preferred_element_type=jnp.float32)
            s = s + bias_scr[p]
            s = jnp.where(pad_mask, NEG_MASK, s)
            m = jnp.max(s, axis=-1, keepdims=True)
            e = jnp.exp(s - m)
            denom = jnp.sum(e, axis=-1, keepdims=True)
            vb = v_scr[pl.ds(r0, BAND), lanes]
            o = jnp.dot(e.astype(jnp.bfloat16), vb, preferred_element_type=jnp.float32)
            o = o / denom
            out = jnp.zeros((CHUNK, PACK_W), jnp.float32)
            for j in range(HEADS_PER_PACK):
                out = out + jnp.where(lane_head == j, o[j * CHUNK:(j + 1) * CHUNK, :], 0.0)
            y = out * gate_scr[pl.ds(r0, CHUNK), lanes]
            yatt_scr[pl.ds(r0, CHUNK), lanes] = y.astype(jnp.bfloat16)
        return carry

    lax.fori_loop(0, chunks_per_tile, chunk_body, 0)

    bg = proj(4)
    u = proj(5) * proj(6)
    u_scr[CARRY_ROWS:CARRY_ROWS + TILE, :] = u
    conv = (convb_ref[...]
            + convw_ref[0:1, :] * u_scr[CARRY_ROWS - 2:CARRY_ROWS - 2 + TILE, :]
            + convw_ref[1:2, :] * u_scr[CARRY_ROWS - 1:CARRY_ROWS - 1 + TILE, :]
            + convw_ref[2:3, :] * u)
    yconv = (bg * conv * _silu(proj(7))).astype(jnp.bfloat16)

    h = (jnp.dot(yatt_scr[...], wout_ref[0:GROUP_W, :], preferred_element_type=jnp.float32)
         + jnp.dot(yconv, wout_ref[GROUP_W:2 * GROUP_W, :], preferred_element_type=jnp.float32))
    o_ref[...] = _layer_norm(ALPHA * xn_scr[...] + h, lng_ref[...], lnb_ref[...])


def _flipped_rel_rows(rel_bias):
    edge = rel_bias[:, 2 * MAX_REL:2 * MAX_REL + 1]
    left = LOOKBACK - MAX_REL
    right = BIAS_LANES - left - (2 * MAX_REL + 1)
    return jnp.concatenate(
        [jnp.broadcast_to(edge, (ATT_HEADS, left)), rel_bias[:, ::-1],
         jnp.broadcast_to(edge, (ATT_HEADS, right))], axis=1)


@jax.jit
def kernel(x, ln0_g, ln0_b, w_in, rel_bias, conv_w, conv_b, w_out, ln_g, ln_b):
    bsz, seq, d = x.shape
    assert d == D_MODEL and seq % TILE == 0 and TILE % CHUNK == 0 and TILE >= LOOKBACK
    f32 = jnp.float32
    row = lambda a: a.reshape(1, -1).astype(f32)
    resident = lambda shape: pl.BlockSpec(shape, lambda b, t: (0,) * len(shape),
                                          pipeline_mode=pl.Buffered(1))
    slab = pl.BlockSpec((None, TILE, D_MODEL), lambda b, t: (b, t, 0))

    return pl.pallas_call(
        _fused_kernel,
        out_shape=jax.ShapeDtypeStruct(x.shape, x.dtype),
        grid=(bsz, seq // TILE),
        in_specs=[
            slab,
            resident((1, D_MODEL)), resident((1, D_MODEL)),
            resident((D_MODEL, 8 * GROUP_W)),
            resident((ATT_HEADS, BIAS_LANES)),
            resident((CONV_WIDTH, GROUP_W)), resident((1, GROUP_W)),
            resident((D_MODEL, D_MODEL)),
            resident((1, D_MODEL)), resident((1, D_MODEL)),
        ],
        out_specs=slab,
        scratch_shapes=[
            pltpu.VMEM((TILE, D_MODEL), f32),
            pltpu.VMEM((TILE, GROUP_W), jnp.bfloat16),
            pltpu.VMEM((LOOKBACK + TILE, GROUP_W), jnp.bfloat16),
            pltpu.VMEM((LOOKBACK + TILE, GROUP_W), jnp.bfloat16),
            pltpu.VMEM((TILE, GROUP_W), f32),
            pltpu.VMEM((TILE, GROUP_W), jnp.bfloat16),
            pltpu.VMEM((CARRY_ROWS + TILE, GROUP_W), f32),
            pltpu.VMEM((NUM_PACKS, PACK_W, BAND), f32),
        ],
        compiler_params=pltpu.CompilerParams(
            dimension_semantics=("arbitrary", "arbitrary"),
            vmem_limit_bytes=VMEM_LIMIT_BYTES),
        name="fused_trunk_layer",
    )(x, row(ln0_g), row(ln0_b), w_in[0].astype(jnp.bfloat16), _flipped_rel_rows(rel_bias[0]),
      conv_w[0], row(conv_b[0]), w_out[0].astype(jnp.bfloat16), row(ln_g[0]), row(ln_b[0]))
```

```python
import functools
import math

import jax
import jax.numpy as jnp
from jax import lax
from jax.experimental import pallas as pl
from jax.experimental.pallas import tpu as pltpu

D_MODEL = 1024
DEPTH = 1
CHUNK = 64
LEFT_CHUNKS = 8
LOOKBACK = LEFT_CHUNKS * CHUNK
BAND = LOOKBACK + CHUNK
GROUP_W = D_MODEL // 2
ATT_HEADS = 8
HEAD_DIM = GROUP_W // ATT_HEADS
CONV_WIDTH = 3
MAX_REL = 128
LN_EPS = 1e-5
ALPHA = (2.0 * DEPTH) ** 0.25
NEG_MASK = -1e30

HEADS_PER_PACK = 4
PACK_W = HEADS_PER_PACK * HEAD_DIM
NUM_PACKS = ATT_HEADS // HEADS_PER_PACK
BIAS_LANES = 1024
CARRY_ROWS = 8

TILE = LOOKBACK
CHUNKS_PER_TILE = TILE // CHUNK
PIPELINE_DRAIN_STEPS = 2
KV_SLOTS = 3
KV_ROWS = (KV_SLOTS + 1) * TILE
LN_ROWS = 64
LN_BLOCKS = TILE // LN_ROWS
COL_W = PACK_W
VMEM_LIMIT_BYTES = 56 * 1024 * 1024


def _layer_norm(v, g, b):
    mu = jnp.mean(v, axis=-1, keepdims=True)
    c = v - mu
    var = jnp.mean(c * c, axis=-1, keepdims=True)
    return c * lax.rsqrt(var + LN_EPS) * g + b


def _silu(z):
    return z * (1.0 / (1.0 + jnp.exp(-z)))


def _fused_kernel(tiles_per_seq,
                  x_ref, ln0g_ref, ln0b_ref, win_ref, relrow_ref, convw_ref, convb_ref,
                  wout_ref, lng_ref, lnb_ref, o_ref,
                  xn_ring, xb_next, xb_cur, q_ring, gate_ring, yconv_ring, k_ring, v_ring,
                  yatt_scr, h_scr, u_scr, bias_scr):
    s = pl.program_id(0)

    @pl.when(s == 0)
    def _first_step():
        for h in range(ATT_HEADS):
            row = jnp.broadcast_to(relrow_ref[h:h + 1, :], (CHUNK, BIAS_LANES))
            toeplitz = pltpu.roll(row, 0, 1, stride=1, stride_axis=0)
            p, j = divmod(h, HEADS_PER_PACK)
            bias_scr[p, j * CHUNK:(j + 1) * CHUNK, :] = toeplitz[:, :BAND]
        for ref in (xn_ring, xb_next, q_ring, gate_ring, yconv_ring, k_ring, v_ring, u_scr):
            ref[...] = jnp.zeros(ref.shape, ref.dtype)

    xb_cur[...] = xb_next[...]

    tile_x = s + 6
    tile_a = s + 5
    tile_b = s + 4
    slot_b = lax.rem(tile_b, 2)
    slot_a = lax.rem(tile_a, 2)
    xn_slot_b = lax.rem(tile_b, 3)
    xn_slot_x = lax.rem(tile_x, 3)

    tile_in_seq_b = lax.rem(jnp.maximum(s - 2, 0), tiles_per_seq)
    kv_base = pl.multiple_of(lax.rem(tile_b - 1, KV_SLOTS) * TILE, TILE)
    lane_head = lax.broadcasted_iota(jnp.int32, (CHUNK, PACK_W), 1) // HEAD_DIM
    col = lax.broadcasted_iota(jnp.int32, (1, BAND), 1)

    for c in range(CHUNKS_PER_TILE):
        rows = slice(c * CHUNK, (c + 1) * CHUNK)
        band = pl.ds(pl.multiple_of(kv_base + c * CHUNK, CHUNK), BAND)
        first_valid = LOOKBACK - (tile_in_seq_b * CHUNKS_PER_TILE + c) * CHUNK
        pad_mask = col < first_valid
        for p in range(NUM_PACKS):
            lanes = slice(p * PACK_W, (p + 1) * PACK_W)
            qc = q_ring[slot_b, rows, lanes]
            qm = jnp.concatenate(
                [jnp.where(lane_head == j, qc, jnp.zeros_like(qc)) for j in range(HEADS_PER_PACK)],
                axis=0)
            sc = lax.dot_general(qm, k_ring[band, lanes], (((1,), (1,)), ((), ())),
                                 preferred_element_type=jnp.float32)
            sc = sc + bias_scr[p]
            sc = jnp.where(pad_mask, NEG_MASK, sc)
            m = jnp.max(sc, axis=-1, keepdims=True)
            e = jnp.exp(sc - m)
            denom = jnp.sum(e, axis=-1, keepdims=True)
            o = jnp.dot(e.astype(jnp.bfloat16), v_ring[band, lanes],
                        preferred_element_type=jnp.float32)
            o = o / denom
            out = jnp.zeros((CHUNK, PACK_W), jnp.float32)
            for j in range(HEADS_PER_PACK):
                out = out + jnp.where(lane_head == j, o[j * CHUNK:(j + 1) * CHUNK, :], 0.0)
            yatt_scr[rows, lanes] = (out * gate_ring[slot_b, rows, lanes]).astype(jnp.bfloat16)

    def entry_norm_block(r):
        rows = slice(r * LN_ROWS, (r + 1) * LN_ROWS)
        xn = _layer_norm(x_ref[rows, :], ln0g_ref[...], ln0b_ref[...])
        xn_ring[xn_slot_x, rows, :] = xn
        xb_next[rows, :] = xn.astype(jnp.bfloat16)

    def post_norm_block(r):
        rows = slice(r * LN_ROWS, (r + 1) * LN_ROWS)
        o_ref[rows, :] = _layer_norm(ALPHA * xn_ring[xn_slot_b, rows, :] + h_scr[rows, :],
                                     lng_ref[...], lnb_ref[...])

    yatt = yatt_scr[...]
    yconv = yconv_ring[slot_b]
    for n in range(D_MODEL // COL_W):
        cols = slice(n * COL_W, (n + 1) * COL_W)
        h_scr[:, cols] = (
            jnp.dot(yatt, wout_ref[0:GROUP_W, cols], preferred_element_type=jnp.float32)
            + jnp.dot(yconv, wout_ref[GROUP_W:2 * GROUP_W, cols], preferred_element_type=jnp.float32))

    tile_in_seq_a = lax.rem(jnp.maximum(s - 1, 0), tiles_per_seq)
    xb = xb_cur[...]
    pieces = GROUP_W // COL_W
    vector_blocks = iter([functools.partial(post_norm_block, r) for r in range(LN_BLOCKS)]
                         + [functools.partial(entry_norm_block, r) for r in range(LN_BLOCKS)])

    def proj(i, n):
        start = i * GROUP_W + n * COL_W
        out = jnp.dot(xb, win_ref[:, start:start + COL_W], preferred_element_type=jnp.float32)
        next(vector_blocks)()
        return out

    kv_slot = lax.rem(tile_a, KV_SLOTS)
    kv_rows = pl.ds(pl.multiple_of(kv_slot * TILE, TILE), TILE)
    kv_dup_rows = pl.ds(pl.multiple_of(jnp.where(kv_slot == 0, KV_SLOTS, kv_slot) * TILE, TILE), TILE)
    carry = u_scr[TILE:TILE + CARRY_ROWS, :]
    u_scr[0:CARRY_ROWS, :] = jnp.where(tile_in_seq_a == 0, 0.0, carry)

    for n in range(pieces):
        cols = slice(n * COL_W, (n + 1) * COL_W)
        q_ring[slot_a, :, cols] = (proj(0, n) * (1.0 / math.sqrt(HEAD_DIM))).astype(jnp.bfloat16)
        k_new = proj(1, n).astype(jnp.bfloat16)
        k_ring[kv_rows, cols] = k_new
        k_ring[kv_dup_rows, cols] = k_new
        v_new = proj(2, n).astype(jnp.bfloat16)
        v_ring[kv_rows, cols] = v_new
        v_ring[kv_dup_rows, cols] = v_new
        gate_ring[slot_a, :, cols] = _silu(proj(3, n))

    for n in range(pieces):
        cols = slice(n * COL_W, (n + 1) * COL_W)
        bg = proj(4, n)
        u = proj(5, n) * proj(6, n)
        u_scr[CARRY_ROWS:CARRY_ROWS + TILE, cols] = u
        conv = (convb_ref[:, cols]
                + convw_ref[0:1, cols] * u_scr[CARRY_ROWS - 2:CARRY_ROWS - 2 + TILE, cols]
                + convw_ref[1:2, cols] * u_scr[CARRY_ROWS - 1:CARRY_ROWS - 1 + TILE, cols]
                + convw_ref[2:3, cols] * u)
        yconv_ring[slot_a, :, cols] = (bg * conv * _silu(proj(7, n))).astype(jnp.bfloat16)


def _flipped_rel_rows(rel_bias):
    edge = rel_bias[:, 2 * MAX_REL:2 * MAX_REL + 1]
    left = LOOKBACK - MAX_REL
    right = BIAS_LANES - left - (2 * MAX_REL + 1)
    return jnp.concatenate(
        [jnp.broadcast_to(edge, (ATT_HEADS, left)), rel_bias[:, ::-1],
         jnp.broadcast_to(edge, (ATT_HEADS, right))], axis=1)


@jax.jit
def kernel(x, ln0_g, ln0_b, w_in, rel_bias, conv_w, conv_b, w_out, ln_g, ln_b):
    bsz, seq, d = x.shape
    assert d == D_MODEL and seq % TILE == 0
    tiles_per_seq = seq // TILE
    n_tiles = bsz * tiles_per_seq
    f32, bf16 = jnp.float32, jnp.bfloat16
    row = lambda a: a.reshape(1, -1).astype(f32)
    resident = lambda shape: pl.BlockSpec(shape, lambda s: (0,) * len(shape),
                                          pipeline_mode=pl.Buffered(1))
    tile_rows = x.reshape(n_tiles, TILE, D_MODEL)

    out = pl.pallas_call(
        functools.partial(_fused_kernel, tiles_per_seq),
        out_shape=jax.ShapeDtypeStruct(tile_rows.shape, x.dtype),
        grid=(n_tiles + PIPELINE_DRAIN_STEPS,),
        in_specs=[
            pl.BlockSpec((None, TILE, D_MODEL), lambda s: (jnp.minimum(s, n_tiles - 1), 0, 0)),
            resident((1, D_MODEL)), resident((1, D_MODEL)),
            resident((D_MODEL, 8 * GROUP_W)),
            resident((ATT_HEADS, BIAS_LANES)),
            resident((CONV_WIDTH, GROUP_W)), resident((1, GROUP_W)),
            resident((D_MODEL, D_MODEL)),
            resident((1, D_MODEL)), resident((1, D_MODEL)),
        ],
        out_specs=pl.BlockSpec((None, TILE, D_MODEL),
                               lambda s: (jnp.maximum(s - PIPELINE_DRAIN_STEPS, 0), 0, 0)),
        scratch_shapes=[
            pltpu.VMEM((3, TILE, D_MODEL), f32),
            pltpu.VMEM((TILE, D_MODEL), bf16),
            pltpu.VMEM((TILE, D_MODEL), bf16),
            pltpu.VMEM((2, TILE, GROUP_W), bf16),
            pltpu.VMEM((2, TILE, GROUP_W), f32),
            pltpu.VMEM((2, TILE, GROUP_W), bf16),
            pltpu.VMEM((KV_ROWS, GROUP_W), bf16),
            pltpu.VMEM((KV_ROWS, GROUP_W), bf16),
            pltpu.VMEM((TILE, GROUP_W), bf16),
            pltpu.VMEM((TILE, D_MODEL), f32),
            pltpu.VMEM((CARRY_ROWS + TILE, GROUP_W), f32),
            pltpu.VMEM((NUM_PACKS, PACK_W, BAND), f32),
        ],
        compiler_params=pltpu.CompilerParams(
            dimension_semantics=("arbitrary",),
            vmem_limit_bytes=VMEM_LIMIT_BYTES),
        name="fused_trunk_layer",
    )(tile_rows, row(ln0_g), row(ln0_b), w_in[0].astype(bf16), _flipped_rel_rows(rel_bias[0]),
      conv_w[0], row(conv_b[0]), w_out[0].astype(bf16), row(ln_g[0]), row(ln_b[0]))
    return out.reshape(x.shape)
```

```python
import collections
import functools
import math

import jax
import jax.numpy as jnp
from jax import lax
from jax.experimental import pallas as pl
from jax.experimental.pallas import tpu as pltpu

D_MODEL = 1024
DEPTH = 1
CHUNK = 64
LEFT_CHUNKS = 8
LOOKBACK = LEFT_CHUNKS * CHUNK
BAND = LOOKBACK + CHUNK
GROUP_W = D_MODEL // 2
ATT_HEADS = 8
HEAD_DIM = GROUP_W // ATT_HEADS
CONV_WIDTH = 3
MAX_REL = 128
LN_EPS = 1e-5
ALPHA = (2.0 * DEPTH) ** 0.25
NEG_MASK = -1e30

HEADS_PER_PACK = 4
PACK_W = HEADS_PER_PACK * HEAD_DIM
NUM_PACKS = ATT_HEADS // HEADS_PER_PACK
BIAS_LANES = 1024
BIAS_START = LOOKBACK - MAX_REL
LOG2_E = math.log2(math.e)
CARRY_ROWS = 8

TILE = LOOKBACK
CHUNKS_PER_TILE = TILE // CHUNK
PIPELINE_DRAIN_STEPS = 3
KV_SLOTS = 3
KV_ROWS = (KV_SLOTS + 1) * TILE
LN_ROWS = 32
LN_BLOCKS = TILE // LN_ROWS
COL_W = PACK_W
SUB_ROWS = 128
OUT_SUB_ROWS = 256
RING_PHASE = 12
VMEM_LIMIT_BYTES = 56 * 1024 * 1024


def _layer_norm(v, g, b):
    mu = jnp.mean(v, axis=-1, keepdims=True)
    c = v - mu
    var = jnp.mean(c * c, axis=-1, keepdims=True)
    return c * lax.rsqrt(var + LN_EPS) * g + b


def _silu(z):
    return z * (1.0 / (1.0 + jnp.exp(-z)))


def _fused_kernel(tiles_per_seq,
                  x_ref, ln0g_ref, ln0b_ref, win_ref, relrow_ref, convw_ref, convb_ref,
                  wout_ref, lng_ref, lnb_ref, o_ref,
                  xn_ring, xb_ring, q_ring, gate_ring, yconv_ring, yatt_ring, k_ring, v_ring,
                  h_scr, u_scr, bias_scr, win_scr):
    s = pl.program_id(0)

    @pl.when(s == 0)
    def _first_step():
        for h in range(ATT_HEADS):
            row = jnp.broadcast_to(relrow_ref[h:h + 1, :], (CHUNK, BIAS_LANES))
            toeplitz = pltpu.roll(row, 0, 1, stride=1, stride_axis=0)
            p, j = divmod(h, HEADS_PER_PACK)
            bias_scr[p, j * CHUNK:(j + 1) * CHUNK, :] = (
                (toeplitz[:, BIAS_START:BAND] - relrow_ref[h:h + 1, 0:1]) * LOG2_E)
        for ref in (xn_ring, xb_ring, q_ring, gate_ring, yconv_ring, yatt_ring, k_ring, v_ring, u_scr):
            ref[...] = jnp.zeros(ref.shape, ref.dtype)
        for i in range(8 * GROUP_W // COL_W):
            win_scr[:, i * COL_W:(i + 1) * COL_W] = win_ref[:, i * COL_W:(i + 1) * COL_W]

    def slot(stage_lag, depth):
        return lax.rem(s + RING_PHASE - stage_lag, depth)

    tile_in_seq_a = lax.rem(jnp.maximum(s - 1, 0), tiles_per_seq)
    tile_in_seq_b = lax.rem(jnp.maximum(s - 2, 0), tiles_per_seq)
    pieces = GROUP_W // COL_W
    vector_work = collections.deque()

    def emit_vector_work():
        if vector_work:
            vector_work.popleft()()

    def entry_norm_block(r):
        rows = slice(r * LN_ROWS, (r + 1) * LN_ROWS)
        xn = _layer_norm(x_ref[rows, :], ln0g_ref[...], ln0b_ref[...])
        xn_ring[slot(0, 4), rows, :] = xn
        xb_ring[slot(0, 2), rows, :] = xn.astype(jnp.bfloat16)

    def proj(i, n, m):
        start = i * GROUP_W + n * COL_W
        out = jnp.dot(xb_ring[slot(1, 2), m * SUB_ROWS:(m + 1) * SUB_ROWS, :],
                      win_scr[:, start:start + COL_W], preferred_element_type=jnp.float32)
        emit_vector_work()
        return out

    def stage_a():
        sub_pieces = range(TILE // SUB_ROWS)
        kv_slot = slot(1, KV_SLOTS)
        kv_base_a = kv_slot * TILE
        kv_dup_base = jnp.where(kv_slot == 0, KV_SLOTS, kv_slot) * TILE

        def kv_store(ring, m, cols, value):
            for base in (kv_base_a, kv_dup_base):
                ring[pl.ds(pl.multiple_of(base + m * SUB_ROWS, SUB_ROWS), SUB_ROWS), cols] = value

        for n in range(pieces):
            cols = slice(n * COL_W, (n + 1) * COL_W)
            for m in sub_pieces:
                rows = slice(m * SUB_ROWS, (m + 1) * SUB_ROWS)
                q_ring[slot(1, 2), rows, cols] = (
                    proj(0, n, m) * (LOG2_E / math.sqrt(HEAD_DIM))).astype(jnp.bfloat16)
            yield
            for m in sub_pieces:
                kv_store(k_ring, m, cols, proj(1, n, m).astype(jnp.bfloat16))
            yield
            for m in sub_pieces:
                kv_store(v_ring, m, cols, proj(2, n, m).astype(jnp.bfloat16))
            yield
            for m in sub_pieces:
                rows = slice(m * SUB_ROWS, (m + 1) * SUB_ROWS)
                gate_ring[slot(1, 2), rows, cols] = _silu(proj(3, n, m))
            yield
        carry = u_scr[TILE:TILE + CARRY_ROWS, :]
        u_scr[0:CARRY_ROWS, :] = jnp.where(tile_in_seq_a == 0, 0.0, carry)
        for n in range(pieces):
            cols = slice(n * COL_W, (n + 1) * COL_W)
            for m in sub_pieces:
                rows = slice(m * SUB_ROWS, (m + 1) * SUB_ROWS)
                first = CARRY_ROWS + m * SUB_ROWS
                bg = proj(4, n, m)
                u = proj(5, n, m) * proj(6, n, m)
                u_scr[first:first + SUB_ROWS, cols] = u
                conv = (convb_ref[:, cols]
                        + convw_ref[0:1, cols] * u_scr[first - 2:first - 2 + SUB_ROWS, cols]
                        + convw_ref[1:2, cols] * u_scr[first - 1:first - 1 + SUB_ROWS, cols]
                        + convw_ref[2:3, cols] * u)
                yconv_ring[slot(1, 3), rows, cols] = (
                    bg * conv * _silu(proj(7, n, m))).astype(jnp.bfloat16)
                yield

    def out_proj_piece(n):
        cols = slice(n * COL_W, (n + 1) * COL_W)
        for m in range(TILE // OUT_SUB_ROWS):
            rows = slice(m * OUT_SUB_ROWS, (m + 1) * OUT_SUB_ROWS)
            att = jnp.dot(yatt_ring[slot(3, 2), rows, :], wout_ref[0:GROUP_W, cols],
                          preferred_element_type=jnp.float32)
            emit_vector_work()
            cnv = jnp.dot(yconv_ring[slot(3, 3), rows, :], wout_ref[GROUP_W:2 * GROUP_W, cols],
                          preferred_element_type=jnp.float32)
            emit_vector_work()
            h_scr[rows, cols] = att + cnv

    def post_norm_block(r):
        rows = slice(r * LN_ROWS, (r + 1) * LN_ROWS)
        o_ref[rows, :] = _layer_norm(ALPHA * xn_ring[slot(3, 4), rows, :] + h_scr[rows, :],
                                     lng_ref[...], lnb_ref[...])

    kv_base = pl.multiple_of(slot(3, KV_SLOTS) * TILE, TILE)
    lane_head = lax.broadcasted_iota(jnp.int32, (CHUNK, PACK_W), 1) // HEAD_DIM
    col = lax.broadcasted_iota(jnp.int32, (1, BAND), 1)

    def scores(c, p):
        rows = slice(c * CHUNK, (c + 1) * CHUNK)
        lanes = slice(p * PACK_W, (p + 1) * PACK_W)
        band = pl.ds(pl.multiple_of(kv_base + c * CHUNK, CHUNK), BAND)
        qc = q_ring[slot(2, 2), rows, lanes]
        qm = jnp.concatenate(
            [jnp.where(lane_head == j, qc, jnp.zeros_like(qc)) for j in range(HEADS_PER_PACK)],
            axis=0)
        return lax.dot_general(qm, k_ring[band, lanes], (((1,), (1,)), ((), ())),
                               preferred_element_type=jnp.float32)

    def softmax_numerator(c, p, j, sc, results):
        rows = slice(j * CHUNK, (j + 1) * CHUNK)
        first_valid = LOOKBACK - (tile_in_seq_b * CHUNKS_PER_TILE + c) * CHUNK
        sc = jnp.concatenate(
            [sc[rows, :BIAS_START], sc[rows, BIAS_START:] + bias_scr[p, rows, :]], axis=1)
        sc = jnp.where(col < first_valid, NEG_MASK, sc)
        m = jnp.max(sc, axis=-1, keepdims=True)
        e = jnp.exp2(sc - m)
        results[p][j] = (e.astype(jnp.bfloat16), jnp.sum(e, axis=-1, keepdims=True))

    def weighted_values(c, p, e):
        lanes = slice(p * PACK_W, (p + 1) * PACK_W)
        band = pl.ds(pl.multiple_of(kv_base + c * CHUNK, CHUNK), BAND)
        return jnp.dot(e, v_ring[band, lanes], preferred_element_type=jnp.float32)

    def finalize(c, p, o, denom):
        rows = slice(c * CHUNK, (c + 1) * CHUNK)
        lanes = slice(p * PACK_W, (p + 1) * PACK_W)
        o = o / denom
        out = o[(HEADS_PER_PACK - 1) * CHUNK:, :]
        for j in reversed(range(HEADS_PER_PACK - 1)):
            out = jnp.where(lane_head == j, o[j * CHUNK:(j + 1) * CHUNK, :], out)
        yatt_ring[slot(2, 2), rows, lanes] = (
            out * gate_ring[slot(2, 2), rows, lanes]).astype(jnp.bfloat16)

    a_pieces = stage_a()
    packs = range(NUM_PACKS)
    out_pieces = D_MODEL // COL_W
    post_per_chunk = LN_BLOCKS // (CHUNKS_PER_TILE - out_pieces)
    entry_per_chunk = LN_BLOCKS // CHUNKS_PER_TILE
    for c in range(CHUNKS_PER_TILE):
        sc = [scores(c, p) for p in packs]
        results = [[None] * HEADS_PER_PACK for _ in packs]
        for p in packs:
            for j in range(HEADS_PER_PACK):
                vector_work.append(functools.partial(softmax_numerator, c, p, j, sc[p], results))
        for r in range(c * entry_per_chunk, (c + 1) * entry_per_chunk):
            vector_work.append(functools.partial(entry_norm_block, r))
        if c >= out_pieces:
            for r in range((c - out_pieces) * post_per_chunk, (c - out_pieces + 1) * post_per_chunk):
                vector_work.append(functools.partial(post_norm_block, r))
        next(a_pieces)
        next(a_pieces)
        if c < out_pieces:
            out_proj_piece(c)
        while vector_work:
            emit_vector_work()
        o = [weighted_values(c, p, jnp.concatenate([e for e, _ in results[p]], axis=0))
             for p in packs]
        for p in packs:
            finalize(c, p, o[p], jnp.concatenate([d for _, d in results[p]], axis=0))


def _flipped_rel_rows(rel_bias):
    edge = rel_bias[:, 2 * MAX_REL:2 * MAX_REL + 1]
    left = LOOKBACK - MAX_REL
    right = BIAS_LANES - left - (2 * MAX_REL + 1)
    return jnp.concatenate(
        [jnp.broadcast_to(edge, (ATT_HEADS, left)), rel_bias[:, ::-1],
         jnp.broadcast_to(edge, (ATT_HEADS, right))], axis=1)


@jax.jit
def kernel(x, ln0_g, ln0_b, w_in, rel_bias, conv_w, conv_b, w_out, ln_g, ln_b):
    bsz, seq, d = x.shape
    assert d == D_MODEL and seq % TILE == 0
    tiles_per_seq = seq // TILE
    n_tiles = bsz * tiles_per_seq
    f32, bf16 = jnp.float32, jnp.bfloat16
    row = lambda a: a.reshape(1, -1).astype(f32)
    resident = lambda shape: pl.BlockSpec(shape, lambda s: (0,) * len(shape),
                                          pipeline_mode=pl.Buffered(1))
    tile_rows = x.reshape(n_tiles, TILE, D_MODEL)

    out = pl.pallas_call(
        functools.partial(_fused_kernel, tiles_per_seq),
        out_shape=jax.ShapeDtypeStruct(tile_rows.shape, x.dtype),
        grid=(n_tiles + PIPELINE_DRAIN_STEPS,),
        in_specs=[
            pl.BlockSpec((None, TILE, D_MODEL), lambda s: (jnp.minimum(s, n_tiles - 1), 0, 0)),
            resident((1, D_MODEL)), resident((1, D_MODEL)),
            resident((D_MODEL, 8 * GROUP_W)),
            resident((ATT_HEADS, BIAS_LANES)),
            resident((CONV_WIDTH, GROUP_W)), resident((1, GROUP_W)),
            resident((D_MODEL, D_MODEL)),
            resident((1, D_MODEL)), resident((1, D_MODEL)),
        ],
        out_specs=pl.BlockSpec((None, TILE, D_MODEL),
                               lambda s: (jnp.maximum(s - PIPELINE_DRAIN_STEPS, 0), 0, 0)),
        scratch_shapes=[
            pltpu.VMEM((4, TILE, D_MODEL), f32),
            pltpu.VMEM((2, TILE, D_MODEL), bf16),
            pltpu.VMEM((2, TILE, GROUP_W), bf16),
            pltpu.VMEM((2, TILE, GROUP_W), f32),
            pltpu.VMEM((3, TILE, GROUP_W), bf16),
            pltpu.VMEM((2, TILE, GROUP_W), bf16),
            pltpu.VMEM((KV_ROWS, GROUP_W), bf16),
            pltpu.VMEM((KV_ROWS, GROUP_W), bf16),
            pltpu.VMEM((TILE, D_MODEL), f32),
            pltpu.VMEM((CARRY_ROWS + TILE, GROUP_W), f32),
            pltpu.VMEM((NUM_PACKS, PACK_W, BAND - BIAS_START), f32),
            pltpu.VMEM((D_MODEL, 8 * GROUP_W), bf16),
        ],
        compiler_params=pltpu.CompilerParams(
            dimension_semantics=("arbitrary",),
            vmem_limit_bytes=VMEM_LIMIT_BYTES),
        name="fused_trunk_layer",
    )(tile_rows, row(ln0_g), row(ln0_b), w_in[0].astype(bf16), _flipped_rel_rows(rel_bias[0]),
      conv_w[0], row(conv_b[0]), w_out[0].astype(bf16), row(ln_g[0]), row(ln_b[0]))
    return out.reshape(x.shape)
```

```python
import functools
import math

import jax
import jax.numpy as jnp
from jax import lax
from jax.experimental import pallas as pl
from jax.experimental.pallas import tpu as pltpu

D_MODEL = 1024
DEPTH = 1
CHUNK = 64
LEFT_CHUNKS = 8
LOOKBACK = LEFT_CHUNKS * CHUNK
BAND = LOOKBACK + CHUNK
GROUP_W = D_MODEL // 2
ATT_HEADS = 8
HEAD_DIM = GROUP_W // ATT_HEADS
CONV_WIDTH = 3
MAX_REL = 128
LN_EPS = 1e-5
ALPHA = (2.0 * DEPTH) ** 0.25
NEG_MASK = -1e30
LOG2_E = math.log2(math.e)

HEADS_PER_PACK = 4
PACK_W = HEADS_PER_PACK * HEAD_DIM
NUM_PACKS = ATT_HEADS // HEADS_PER_PACK
BIAS_LANES = 1024
BIAS_START = LOOKBACK - MAX_REL
CARRY_ROWS = 8

TILE = LOOKBACK
CHUNKS_PER_TILE = TILE // CHUNK
PIPELINE_DRAIN_STEPS = 3
KV_SLOTS = 3
KV_ROWS = (KV_SLOTS + 1) * TILE
LN_ROWS = 64
LN_BLOCKS = TILE // LN_ROWS
COL_W = PACK_W
RING_PHASE = 12
VMEM_LIMIT_BYTES = 60000 * 1024


def _layer_norm(v, g, b):
    mu = jnp.mean(v, axis=-1, keepdims=True)
    c = v - mu
    var = jnp.mean(c * c, axis=-1, keepdims=True)
    return c * lax.rsqrt(var + LN_EPS) * g + b


def _silu(z):
    return z * (1.0 / (1.0 + jnp.exp(-z)))


def _fused_kernel(tiles_per_seq,
                  x_ref, ln0g_ref, ln0b_ref, win_ref, relrow_ref, convw_ref, convb_ref,
                  wout_ref, lng_ref, lnb_ref, o_ref,
                  xn_ring, xb_next, xb_cur, q_ring, gate_ring, yconv_ring, k_ring, v_ring,
                  yatt_scr, h_scr, u_scr, bias_scr, win_scr):
    s = pl.program_id(0)

    @pl.when(s == 0)
    def _first_step():
        for h in range(ATT_HEADS):
            row = jnp.broadcast_to(relrow_ref[h:h + 1, :], (CHUNK, BIAS_LANES))
            toeplitz = pltpu.roll(row, 0, 1, stride=1, stride_axis=0)
            p, j = divmod(h, HEADS_PER_PACK)
            bias_scr[p, j * CHUNK:(j + 1) * CHUNK, :] = (
                (toeplitz[:, BIAS_START:BAND] - relrow_ref[h:h + 1, 0:1]) * LOG2_E)
        for ref in (xn_ring, xb_next, q_ring, gate_ring, yconv_ring, k_ring, v_ring, u_scr, h_scr):
            ref[...] = jnp.zeros(ref.shape, ref.dtype)
        for i in range(8 * GROUP_W // COL_W):
            win_scr[:, i * COL_W:(i + 1) * COL_W] = win_ref[:, i * COL_W:(i + 1) * COL_W]

    xb_cur[...] = xb_next[...]

    def slot(stage_lag, depth):
        return lax.rem(s + RING_PHASE - stage_lag, depth)

    tile_in_seq_a = lax.rem(jnp.maximum(s - 1, 0), tiles_per_seq)
    tile_in_seq_b = lax.rem(jnp.maximum(s - 2, 0), tiles_per_seq)
    pieces = GROUP_W // COL_W

    kv_base = pl.multiple_of(slot(3, KV_SLOTS) * TILE, TILE)
    lane_head = lax.broadcasted_iota(jnp.int32, (CHUNK, PACK_W), 1) // HEAD_DIM
    col = lax.broadcasted_iota(jnp.int32, (1, BAND), 1)

    def attention_block(c, p):
        rows = slice(c * CHUNK, (c + 1) * CHUNK)
        lanes = slice(p * PACK_W, (p + 1) * PACK_W)
        band = pl.ds(pl.multiple_of(kv_base + c * CHUNK, CHUNK), BAND)
        qc = q_ring[slot(2, 2), rows, lanes]
        qm = jnp.concatenate(
            [jnp.where(lane_head == j, qc, jnp.zeros_like(qc)) for j in range(HEADS_PER_PACK)],
            axis=0)
        sc = lax.dot_general(qm, k_ring[band, lanes], (((1,), (1,)), ((), ())),
                             preferred_element_type=jnp.float32)
        first_valid = LOOKBACK - (tile_in_seq_b * CHUNKS_PER_TILE + c) * CHUNK
        sc = jnp.concatenate([sc[:, :BIAS_START], sc[:, BIAS_START:] + bias_scr[p]], axis=1)
        sc = jnp.where(col < first_valid, NEG_MASK, sc)
        m = jnp.max(sc, axis=-1, keepdims=True)
        e = jnp.exp2(sc - m)
        denom = jnp.sum(e, axis=-1, keepdims=True)
        o = jnp.dot(e.astype(jnp.bfloat16), v_ring[band, lanes],
                    preferred_element_type=jnp.float32)
        o = o / denom
        out = o[(HEADS_PER_PACK - 1) * CHUNK:, :]
        for j in reversed(range(HEADS_PER_PACK - 1)):
            out = jnp.where(lane_head == j, o[j * CHUNK:(j + 1) * CHUNK, :], out)
        yatt_scr[rows, lanes] = (out * gate_ring[slot(2, 2), rows, lanes]).astype(jnp.bfloat16)

    for c in range(CHUNKS_PER_TILE):
        for p in range(NUM_PACKS):
            attention_block(c, p)

    for r in range(LN_BLOCKS):
        rows = slice(r * LN_ROWS, (r + 1) * LN_ROWS)
        o_ref[rows, :] = _layer_norm(ALPHA * xn_ring[slot(3, 4), rows, :] + h_scr[rows, :],
                                     lng_ref[...], lnb_ref[...])

    yatt = yatt_scr[...]
    yconv = yconv_ring[slot(2, 2)]
    for n in range(D_MODEL // COL_W):
        cols = slice(n * COL_W, (n + 1) * COL_W)
        h_scr[:, cols] = (
            jnp.dot(yatt, wout_ref[0:GROUP_W, cols], preferred_element_type=jnp.float32)
            + jnp.dot(yconv, wout_ref[GROUP_W:2 * GROUP_W, cols], preferred_element_type=jnp.float32))

    for r in range(LN_BLOCKS):
        rows = slice(r * LN_ROWS, (r + 1) * LN_ROWS)
        xn = _layer_norm(x_ref[rows, :], ln0g_ref[...], ln0b_ref[...])
        xn_ring[slot(0, 4), rows, :] = xn
        xb_next[rows, :] = xn.astype(jnp.bfloat16)

    xb = xb_cur[...]

    def proj(i, n):
        start = i * GROUP_W + n * COL_W
        return jnp.dot(xb, win_scr[:, start:start + COL_W], preferred_element_type=jnp.float32)

    kv_slot = slot(1, KV_SLOTS)
    kv_rows = pl.ds(pl.multiple_of(kv_slot * TILE, TILE), TILE)
    kv_dup_rows = pl.ds(pl.multiple_of(jnp.where(kv_slot == 0, KV_SLOTS, kv_slot) * TILE, TILE), TILE)
    carry = u_scr[TILE:TILE + CARRY_ROWS, :]
    u_scr[0:CARRY_ROWS, :] = jnp.where(tile_in_seq_a == 0, 0.0, carry)

    for n in range(pieces):
        cols = slice(n * COL_W, (n + 1) * COL_W)
        bg = proj(4, n)
        u = proj(5, n) * proj(6, n)
        u_scr[CARRY_ROWS:CARRY_ROWS + TILE, cols] = u
        conv = (convb_ref[:, cols]
                + convw_ref[0:1, cols] * u_scr[CARRY_ROWS - 2:CARRY_ROWS - 2 + TILE, cols]
                + convw_ref[1:2, cols] * u_scr[CARRY_ROWS - 1:CARRY_ROWS - 1 + TILE, cols]
                + convw_ref[2:3, cols] * u)
        yconv_ring[slot(1, 2), :, cols] = (bg * conv * _silu(proj(7, n))).astype(jnp.bfloat16)

    for n in range(pieces):
        cols = slice(n * COL_W, (n + 1) * COL_W)
        gate_ring[slot(1, 2), :, cols] = _silu(proj(3, n))
        q_ring[slot(1, 2), :, cols] = (
            proj(0, n) * (LOG2_E / math.sqrt(HEAD_DIM))).astype(jnp.bfloat16)
        k_new = proj(1, n).astype(jnp.bfloat16)
        k_ring[kv_rows, cols] = k_new
        k_ring[kv_dup_rows, cols] = k_new
        v_new = proj(2, n).astype(jnp.bfloat16)
        v_ring[kv_rows, cols] = v_new
        v_ring[kv_dup_rows, cols] = v_new


def _flipped_rel_rows(rel_bias):
    edge = rel_bias[:, 2 * MAX_REL:2 * MAX_REL + 1]
    left = LOOKBACK - MAX_REL
    right = BIAS_LANES - left - (2 * MAX_REL + 1)
    return jnp.concatenate(
        [jnp.broadcast_to(edge, (ATT_HEADS, left)), rel_bias[:, ::-1],
         jnp.broadcast_to(edge, (ATT_HEADS, right))], axis=1)


@jax.jit
def kernel(x, ln0_g, ln0_b, w_in, rel_bias, conv_w, conv_b, w_out, ln_g, ln_b):
    bsz, seq, d = x.shape
    assert d == D_MODEL and seq % TILE == 0
    tiles_per_seq = seq // TILE
    n_tiles = bsz * tiles_per_seq
    f32, bf16 = jnp.float32, jnp.bfloat16
    row = lambda a: a.reshape(1, -1).astype(f32)
    resident = lambda shape: pl.BlockSpec(shape, lambda s: (0,) * len(shape),
                                          pipeline_mode=pl.Buffered(1))
    tile_rows = x.reshape(n_tiles, TILE, D_MODEL)

    out = pl.pallas_call(
        functools.partial(_fused_kernel, tiles_per_seq),
        out_shape=jax.ShapeDtypeStruct(tile_rows.shape, x.dtype),
        grid=(n_tiles + PIPELINE_DRAIN_STEPS,),
        in_specs=[
            pl.BlockSpec((None, TILE, D_MODEL), lambda s: (jnp.minimum(s, n_tiles - 1), 0, 0)),
            resident((1, D_MODEL)), resident((1, D_MODEL)),
            resident((D_MODEL, 8 * GROUP_W)),
            resident((ATT_HEADS, BIAS_LANES)),
            resident((CONV_WIDTH, GROUP_W)), resident((1, GROUP_W)),
            resident((D_MODEL, D_MODEL)),
            resident((1, D_MODEL)), resident((1, D_MODEL)),
        ],
        out_specs=pl.BlockSpec((None, TILE, D_MODEL),
                               lambda s: (jnp.maximum(s - PIPELINE_DRAIN_STEPS, 0), 0, 0)),
        scratch_shapes=[
            pltpu.VMEM((4, TILE, D_MODEL), f32),
            pltpu.VMEM((TILE, D_MODEL), bf16),
            pltpu.VMEM((TILE, D_MODEL), bf16),
            pltpu.VMEM((2, TILE, GROUP_W), bf16),
            pltpu.VMEM((2, TILE, GROUP_W), f32),
            pltpu.VMEM((2, TILE, GROUP_W), bf16),
            pltpu.VMEM((KV_ROWS, GROUP_W), bf16),
            pltpu.VMEM((KV_ROWS, GROUP_W), bf16),
            pltpu.VMEM((TILE, GROUP_W), bf16),
            pltpu.VMEM((TILE, D_MODEL), f32),
            pltpu.VMEM((CARRY_ROWS + TILE, GROUP_W), f32),
            pltpu.VMEM((NUM_PACKS, PACK_W, BAND - BIAS_START), f32),
            pltpu.VMEM((D_MODEL, 8 * GROUP_W), bf16),
        ],
        compiler_params=pltpu.CompilerParams(
            dimension_semantics=("arbitrary",),
            vmem_limit_bytes=VMEM_LIMIT_BYTES),
        name="fused_trunk_layer",
    )(tile_rows, row(ln0_g), row(ln0_b), w_in[0].astype(bf16), _flipped_rel_rows(rel_bias[0]),
      conv_w[0], row(conv_b[0]), w_out[0].astype(bf16), row(ln_g[0]), row(ln_b[0]))
    return out.reshape(x.shape)
```

```python
import functools
import math

import jax
import jax.numpy as jnp
from jax import lax
from jax.experimental import pallas as pl
from jax.experimental.pallas import tpu as pltpu

D_MODEL = 1024
DEPTH = 1
CHUNK = 64
LEFT_CHUNKS = 8
LOOKBACK = LEFT_CHUNKS * CHUNK
BAND = LOOKBACK + CHUNK
GROUP_W = D_MODEL // 2
ATT_HEADS = 8
HEAD_DIM = GROUP_W // ATT_HEADS
CONV_WIDTH = 3
MAX_REL = 128
LN_EPS = 1e-5
ALPHA = (2.0 * DEPTH) ** 0.25
NEG_MASK = -1e30
LOG2_E = math.log2(math.e)

HEADS_PER_PACK = 4
PACK_W = HEADS_PER_PACK * HEAD_DIM
NUM_PACKS = ATT_HEADS // HEADS_PER_PACK
BIAS_LANES = 1024
BIAS_START = LOOKBACK - MAX_REL
CARRY_ROWS = 8

TILE = LOOKBACK
CHUNKS_PER_TILE = TILE // CHUNK
PIPELINE_DRAIN_STEPS = 3
KV_SLOTS = 3
KV_ROWS = (KV_SLOTS + 1) * TILE
LN_ROWS = 64
LN_BLOCKS = TILE // LN_ROWS
COL_W = PACK_W
RING_PHASE = 12
VMEM_LIMIT_BYTES = 60000 * 1024


def _layer_norm(v, g, b):
    mu = jnp.mean(v, axis=-1, keepdims=True)
    c = v - mu
    var = jnp.mean(c * c, axis=-1, keepdims=True)
    return c * lax.rsqrt(var + LN_EPS) * g + b


def _silu(z):
    return z * (1.0 / (1.0 + jnp.exp(-z)))


def _fused_kernel(tiles_per_seq,
                  x_ref, ln0g_ref, ln0b_ref, win_hbm, relrow_ref, convw_ref, convb_ref,
                  wout_hbm, lng_ref, lnb_ref, o_ref,
                  xn_ring, xb_ring, q_ring, gate_ring, yconv_ring, k_ring, v_ring,
                  yatt_scr, h_scr, u_scr, bias_scr, win_scr, wout_scr, stage_scr, stage_sem):
    s = pl.program_id(0)

    @pl.when(s == 0)
    def _first_step():
        for h in range(ATT_HEADS):
            row = jnp.broadcast_to(relrow_ref[h:h + 1, :], (CHUNK, BIAS_LANES))
            toeplitz = pltpu.roll(row, 0, 1, stride=1, stride_axis=0)
            p, j = divmod(h, HEADS_PER_PACK)
            bias_scr[p, j * CHUNK:(j + 1) * CHUNK, :] = (
                (toeplitz[:, BIAS_START:BAND] - relrow_ref[h:h + 1, 0:1]) * LOG2_E)
        for ref in (xn_ring, xb_ring, q_ring, gate_ring, yconv_ring, k_ring, v_ring, u_scr, h_scr):
            ref[...] = jnp.zeros(ref.shape, ref.dtype)
        for hbm, vmem in ((win_hbm, win_scr), (wout_hbm, wout_scr)):
            for i in range(hbm.shape[1] // COL_W):
                cols = slice(i * COL_W, (i + 1) * COL_W)
                fetch = pltpu.make_async_copy(hbm.at[:, cols], stage_scr, stage_sem)
                fetch.start()
                fetch.wait()
                vmem[:, cols] = stage_scr[...].astype(jnp.bfloat16)

    def slot(stage_lag, depth):
        return lax.rem(s + RING_PHASE - stage_lag, depth)

    tile_in_seq_a = lax.rem(jnp.maximum(s - 1, 0), tiles_per_seq)
    tile_in_seq_b = lax.rem(jnp.maximum(s - 2, 0), tiles_per_seq)
    pieces = GROUP_W // COL_W

    kv_base = pl.multiple_of(slot(3, KV_SLOTS) * TILE, TILE)
    lane_head = lax.broadcasted_iota(jnp.int32, (CHUNK, PACK_W), 1) // HEAD_DIM
    col = lax.broadcasted_iota(jnp.int32, (1, BAND), 1)

    def attention_block(c, p):
        rows = slice(c * CHUNK, (c + 1) * CHUNK)
        lanes = slice(p * PACK_W, (p + 1) * PACK_W)
        band = pl.ds(pl.multiple_of(kv_base + c * CHUNK, CHUNK), BAND)
        qc = q_ring[slot(2, 2), rows, lanes]
        qm = jnp.concatenate(
            [jnp.where(lane_head == j, qc, jnp.zeros_like(qc)) for j in range(HEADS_PER_PACK)],
            axis=0)
        sc = lax.dot_general(qm, k_ring[band, lanes], (((1,), (1,)), ((), ())),
                             preferred_element_type=jnp.float32)
        first_valid = LOOKBACK - (tile_in_seq_b * CHUNKS_PER_TILE + c) * CHUNK
        sc = jnp.concatenate([sc[:, :BIAS_START], sc[:, BIAS_START:] + bias_scr[p]], axis=1)
        sc = jnp.where(col < first_valid, NEG_MASK, sc)
        m = jnp.max(sc, axis=-1, keepdims=True)
        e = jnp.exp2(sc - m)
        denom = jnp.sum(e, axis=-1, keepdims=True)
        o = jnp.dot(e.astype(jnp.bfloat16), v_ring[band, lanes],
                    preferred_element_type=jnp.float32)
        o = o / denom
        out = o[(HEADS_PER_PACK - 1) * CHUNK:, :]
        for j in reversed(range(HEADS_PER_PACK - 1)):
            out = jnp.where(lane_head == j, o[j * CHUNK:(j + 1) * CHUNK, :], out)
        yatt_scr[rows, lanes] = (out * gate_ring[slot(2, 2), rows, lanes]).astype(jnp.bfloat16)

    for c in range(CHUNKS_PER_TILE):
        for p in range(NUM_PACKS):
            attention_block(c, p)

    def proj(i, n):
        start = i * GROUP_W + n * COL_W
        return jnp.dot(xb_ring[slot(1, 2)], win_scr[:, start:start + COL_W],
                       preferred_element_type=jnp.float32)

    kv_slot = slot(1, KV_SLOTS)
    kv_rows = pl.ds(pl.multiple_of(kv_slot * TILE, TILE), TILE)
    kv_dup_rows = pl.ds(pl.multiple_of(jnp.where(kv_slot == 0, KV_SLOTS, kv_slot) * TILE, TILE), TILE)
    carry = u_scr[TILE:TILE + CARRY_ROWS, :]
    u_scr[0:CARRY_ROWS, :] = jnp.where(tile_in_seq_a == 0, 0.0, carry)

    for n in range(pieces):
        cols = slice(n * COL_W, (n + 1) * COL_W)
        bg = proj(4, n)
        u = proj(5, n) * proj(6, n)
        u_scr[CARRY_ROWS:CARRY_ROWS + TILE, cols] = u
        conv = (convb_ref[:, cols]
                + convw_ref[0:1, cols] * u_scr[CARRY_ROWS - 2:CARRY_ROWS - 2 + TILE, cols]
                + convw_ref[1:2, cols] * u_scr[CARRY_ROWS - 1:CARRY_ROWS - 1 + TILE, cols]
                + convw_ref[2:3, cols] * u)
        yconv_ring[slot(1, 2), :, cols] = (bg * conv * _silu(proj(7, n))).astype(jnp.bfloat16)

    for n in range(pieces):
        cols = slice(n * COL_W, (n + 1) * COL_W)
        gate_ring[slot(1, 2), :, cols] = _silu(proj(3, n))
        q_ring[slot(1, 2), :, cols] = (
            proj(0, n) * (LOG2_E / math.sqrt(HEAD_DIM))).astype(jnp.bfloat16)
        k_new = proj(1, n).astype(jnp.bfloat16)
        k_ring[kv_rows, cols] = k_new
        k_ring[kv_dup_rows, cols] = k_new
        v_new = proj(2, n).astype(jnp.bfloat16)
        v_ring[kv_rows, cols] = v_new
        v_ring[kv_dup_rows, cols] = v_new

    for r in range(LN_BLOCKS):
        rows = slice(r * LN_ROWS, (r + 1) * LN_ROWS)
        o_ref[rows, :] = _layer_norm(ALPHA * xn_ring[slot(3, 4), rows, :] + h_scr[rows, :],
                                     lng_ref[...], lnb_ref[...])

    for r in range(LN_BLOCKS):
        rows = slice(r * LN_ROWS, (r + 1) * LN_ROWS)
        xn = _layer_norm(x_ref[rows, :], ln0g_ref[...], ln0b_ref[...])
        xn_ring[slot(0, 4), rows, :] = xn
        xb_ring[slot(0, 2), rows, :] = xn.astype(jnp.bfloat16)

    yatt = yatt_scr[...]
    yconv = yconv_ring[slot(2, 2)]
    for n in range(D_MODEL // COL_W):
        cols = slice(n * COL_W, (n + 1) * COL_W)
        h_scr[:, cols] = (
            jnp.dot(yatt, wout_scr[0:GROUP_W, cols], preferred_element_type=jnp.float32)
            + jnp.dot(yconv, wout_scr[GROUP_W:2 * GROUP_W, cols], preferred_element_type=jnp.float32))


def _flipped_rel_rows(rel_bias):
    edge = rel_bias[:, 2 * MAX_REL:2 * MAX_REL + 1]
    left = LOOKBACK - MAX_REL
    right = BIAS_LANES - left - (2 * MAX_REL + 1)
    return jnp.concatenate(
        [jnp.broadcast_to(edge, (ATT_HEADS, left)), rel_bias[:, ::-1],
         jnp.broadcast_to(edge, (ATT_HEADS, right))], axis=1)


@jax.jit
def kernel(x, ln0_g, ln0_b, w_in, rel_bias, conv_w, conv_b, w_out, ln_g, ln_b):
    bsz, seq, d = x.shape
    assert d == D_MODEL and seq % TILE == 0
    tiles_per_seq = seq // TILE
    n_tiles = bsz * tiles_per_seq
    f32, bf16 = jnp.float32, jnp.bfloat16
    row = lambda a: a.reshape(1, -1).astype(f32)
    resident = lambda shape: pl.BlockSpec(shape, lambda s: (0,) * len(shape),
                                          pipeline_mode=pl.Buffered(1))
    tile_rows = x.reshape(n_tiles, TILE, D_MODEL)

    out = pl.pallas_call(
        functools.partial(_fused_kernel, tiles_per_seq),
        out_shape=jax.ShapeDtypeStruct(tile_rows.shape, x.dtype),
        grid=(n_tiles + PIPELINE_DRAIN_STEPS,),
        in_specs=[
            pl.BlockSpec((None, TILE, D_MODEL), lambda s: (jnp.minimum(s, n_tiles - 1), 0, 0)),
            resident((1, D_MODEL)), resident((1, D_MODEL)),
            pl.BlockSpec(memory_space=pl.ANY),
            resident((ATT_HEADS, BIAS_LANES)),
            resident((CONV_WIDTH, GROUP_W)), resident((1, GROUP_W)),
            pl.BlockSpec(memory_space=pl.ANY),
            resident((1, D_MODEL)), resident((1, D_MODEL)),
        ],
        out_specs=pl.BlockSpec((None, TILE, D_MODEL),
                               lambda s: (jnp.maximum(s - PIPELINE_DRAIN_STEPS, 0), 0, 0)),
        scratch_shapes=[
            pltpu.VMEM((4, TILE, D_MODEL), f32),
            pltpu.VMEM((2, TILE, D_MODEL), bf16),
            pltpu.VMEM((2, TILE, GROUP_W), bf16),
            pltpu.VMEM((2, TILE, GROUP_W), f32),
            pltpu.VMEM((2, TILE, GROUP_W), bf16),
            pltpu.VMEM((KV_ROWS, GROUP_W), bf16),
            pltpu.VMEM((KV_ROWS, GROUP_W), bf16),
            pltpu.VMEM((TILE, GROUP_W), bf16),
            pltpu.VMEM((TILE, D_MODEL), f32),
            pltpu.VMEM((CARRY_ROWS + TILE, GROUP_W), f32),
            pltpu.VMEM((NUM_PACKS, PACK_W, BAND - BIAS_START), f32),
            pltpu.VMEM((D_MODEL, 8 * GROUP_W), bf16),
            pltpu.VMEM((D_MODEL, D_MODEL), bf16),
            pltpu.VMEM((D_MODEL, COL_W), f32),
            pltpu.SemaphoreType.DMA(()),
        ],
        compiler_params=pltpu.CompilerParams(
            dimension_semantics=("arbitrary",),
            vmem_limit_bytes=VMEM_LIMIT_BYTES),
        name="fused_trunk_layer",
    )(tile_rows, row(ln0_g), row(ln0_b), w_in[0], _flipped_rel_rows(rel_bias[0]),
      conv_w[0], row(conv_b[0]), w_out[0], row(ln_g[0]), row(ln_b[0]))
    return out.reshape(x.shape)
```
